```python
import jax
import jax.numpy as jnp
from jax import lax
import numpy as np

D_MODEL = 1024
BATCH = 8
SEQ = 4096
DEPTH = 1

CTX_LEN = 256
GRID_W = 64

RW_HEADS = 8
RW_HEAD_DIM = 64
RW_WIDTH = RW_HEADS * RW_HEAD_DIM
RW_DECAY_RANK = 64
RW_ICLR_RANK = 64
RW_GATE_RANK = 128
RW_GN_EPS = 64e-5
RW_COLS = 3 * RW_WIDTH + 2 * RW_DECAY_RANK + 2 * RW_ICLR_RANK + RW_GATE_RANK
RW_SPLITS = tuple(int(i) for i in np.cumsum([RW_WIDTH] * 3 + [RW_DECAY_RANK] * 2 + [RW_ICLR_RANK] * 2))

GLA_HEADS = 4
GLA_KEY_DIM = 64
GLA_VAL_DIM = 128
GLA_QK_WIDTH = GLA_HEADS * GLA_KEY_DIM
GLA_WIDTH = GLA_HEADS * GLA_VAL_DIM
GLA_GATE_RANK = 16
GLA_GATE_TAU = 16.0
GLA_CHUNK = 64
GLA_SUBCHUNK = 16
GLA_NORM_EPS = 1e-5
GLA_COLS = 2 * GLA_QK_WIDTH + GLA_WIDTH + 2 * GLA_GATE_RANK + GLA_WIDTH
GLA_SPLITS = tuple(int(i) for i in np.cumsum([GLA_QK_WIDTH] * 2 + [GLA_WIDTH] + [GLA_GATE_RANK] * 2))

MIX_WIDTH = RW_WIDTH + GLA_WIDTH
N_IN = RW_COLS + GLA_COLS
FFN_HIDDEN = ((-(-8 * D_MODEL // 3)) + 255) // 256 * 256
NORM_EPS = 1e-6

kernel_name = "hybrid_rwkv7_gla_dit_block"


def _rmsnorm(x, g):
    xf = x.astype(jnp.float32)
    y = xf * lax.rsqrt(jnp.mean(xf * xf, axis=-1, keepdims=True) + NORM_EPS)
    return (y * g.astype(jnp.float32)).astype(x.dtype)


def _modulate(h, shift, scale):
    return h * (1 + scale) + shift


def _swiglu(u, w_in, w_out):
    gate, up = jnp.split(u @ w_in, 2, axis=-1)
    return (jax.nn.silu(gate) * up) @ w_out


def _grid_transpose(t, a, b):
    B = t.shape[0]
    return t.reshape((B, a, b) + t.shape[2:]).swapaxes(1, 2).reshape(t.shape)


def _qshift_grid(p, rows):
    B, T, C = p.shape
    p5 = p.reshape(B, rows, GRID_W, C // 4, 4)
    zc = jnp.zeros_like(p5[:, :, :1, :, 0])
    zr = jnp.zeros_like(p5[:, :1, :, :, 0])
    left = jnp.concatenate([zc, p5[:, :, :-1, :, 0]], axis=2)
    right = jnp.concatenate([p5[:, :, 1:, :, 1], zc], axis=2)
    up = jnp.concatenate([zr, p5[:, :-1, :, :, 2]], axis=1)
    down = jnp.concatenate([p5[:, 1:, :, :, 3], zr], axis=1)
    return jnp.stack([left, right, up, down], axis=-1).reshape(B, T, C)


def _qshift_seq(p):
    B, T, C = p.shape
    p4 = p.reshape(B, T, C // 4, 4)
    z = jnp.zeros_like(p4[:, :1])
    prev = jnp.concatenate([z, p4[:, :-1]], axis=1)
    nxt = jnp.concatenate([p4[:, 1:], z], axis=1)
    even = (jnp.arange(4) % 2) == 0
    return jnp.where(even, prev, nxt).reshape(B, T, C)


def _rwkv_inputs(p, p_shift, mu, w0, w2, a0, a2, k_k, k_a):
    xm = p + (p_shift - p) * mu
    r, k, v, lw_f, lw_b, la_f, la_b, lg = jnp.split(xm, RW_SPLITS, axis=-1)
    B, T, _ = p.shape
    heads = lambda t: t.reshape(B, T, RW_HEADS, RW_HEAD_DIM)
    kk = heads((k * k_k).astype(jnp.float32))
    kk = kk / jnp.maximum(jnp.linalg.norm(kk, axis=-1, keepdims=True), 1e-12)
    w_dirs, k_dirs, a_dirs = [], [], []
    for d, (lw, la) in enumerate(((lw_f, la_f), (lw_b, la_b))):
        w_raw = -jax.nn.softplus(-(w0[d] + jnp.tanh(lw) @ w2[d])) - 0.5
        w_dirs.append(heads(jnp.exp(-jnp.exp(w_raw.astype(jnp.float32)))))
        a = jax.nn.sigmoid(a0[d] + la @ a2[d])
        k_dirs.append(heads(k * (1 + (a - 1) * k_a)))
        a_dirs.append(heads(a))
    return dict(r=heads(r), v=heads(v), kk=kk, lg=lg, w=w_dirs, k=k_dirs, a=a_dirs)


def _rwkv7_scan(s0, r, w, k, v, kk, a, reverse, emit):
    xs = tuple(jnp.moveaxis(t.astype(jnp.float32), 1, 0) for t in (r, w, k, v, kk, a))

    def step(s, inp):
        r_t, w_t, k_t, v_t, kk_t, a_t = inp
        sa = jnp.einsum('bhvk,bhk->bhv', s, -kk_t)
        s = (s * w_t[:, :, None, :] + sa[..., None] * (kk_t * a_t)[:, :, None, :]
             + v_t[..., None] * k_t[:, :, None, :])
        return s, (jnp.einsum('bhvk,bhk->bhv', s, r_t) if emit else None)

    s, ys = lax.scan(step, s0, xs, reverse=reverse)
    return s, (jnp.moveaxis(ys, 0, 1) if emit else None)


def _rwkv_output(y, f, r_k, ln_w, ln_b, g2):
    B, T = y.shape[:2]
    mean = jnp.mean(y, axis=-1, keepdims=True)
    var = jnp.mean(jnp.square(y - mean), axis=-1, keepdims=True)
    yn = ((y - mean) * lax.rsqrt(var + RW_GN_EPS)).reshape(B, T, RW_WIDTH) * ln_w + ln_b
    bonus = jnp.sum(f['r'] * (f['k'][0] + f['k'][1]) * r_k, axis=-1, keepdims=True) * f['v']
    g = jax.nn.sigmoid(f['lg']) @ g2
    return (yn + bonus.reshape(B, T, RW_WIDTH)) * g


def _gla_inputs(p, a2, ab):
    q, k, v, lf, lb, og = jnp.split(p, GLA_SPLITS, axis=-1)
    B, T, _ = p.shape
    hk = lambda t: t.reshape(B, T, GLA_HEADS, GLA_KEY_DIM)
    log_a = [hk(jax.nn.log_sigmoid((lr @ a2[d] + ab[d]).astype(jnp.float32)) / GLA_GATE_TAU)
             for d, lr in enumerate((lf, lb))]
    return dict(q=hk(q * GLA_KEY_DIM ** -0.5), k=hk(k),
                v=v.reshape(B, T, GLA_HEADS, GLA_VAL_DIM), log_a=log_a, og=og)


def _gla_intra(q, k, b):
    B, H, C, DK = q.shape
    L = GLA_SUBCHUNK
    S = C // L
    qs, ks, bs = (t.reshape(B, H, S, L, DK) for t in (q, k, b))
    causal = jnp.tril(jnp.ones((L, L), bool))
    diff = bs[:, :, :, :, None, :] - bs[:, :, :, None, :, :]
    decay = jnp.exp(jnp.where(causal[:, :, None], diff, -jnp.inf))
    a_diag = jnp.einsum('bhsid,bhsjd,bhsijd->bhsij', qs, ks, decay)
    e = bs[:, :, :, -1, :]
    earlier = jnp.arange(S)[:, None] > jnp.arange(S)[None, :]
    q_exp = jnp.where(earlier[:, :, None, None],
                      bs[:, :, :, None] - e[:, :, None, :, None, :], -jnp.inf)
    q_off = qs[:, :, :, None] * jnp.exp(q_exp)
    k_off = ks * jnp.exp(e[:, :, :, None, :] - bs)
    a_off = jnp.einsum('bhpsid,bhsjd->bhpsij', q_off, k_off)
    a_blk = a_off + jnp.eye(S, dtype=a_off.dtype)[:, :, None, None] * a_diag[:, :, :, None]
    return a_blk.transpose(0, 1, 2, 4, 3, 5).reshape(B, H, C, C)


def _gla_scan(s0, q, k, v, log_a, reverse, emit):
    if reverse:
        q, k, v, log_a = (jnp.flip(t, axis=1) for t in (q, k, v, log_a))
    B, T, H, _ = k.shape
    n = T // GLA_CHUNK

    def chunks(t):
        t = t.astype(jnp.float32).reshape(B, n, GLA_CHUNK, H, t.shape[-1])
        return t.transpose(1, 0, 3, 2, 4)

    def step(s, inp):
        qc, kc, vc, gc = inp
        b = jnp.cumsum(gc, axis=2)
        b_last = b[:, :, -1:, :]
        o = None
        if emit:
            o = (jnp.einsum('bhcd,bhde->bhce', qc * jnp.exp(b), s)
                 + jnp.einsum('bhij,bhje->bhie', _gla_intra(qc, kc, b), vc))
        s = (jnp.exp(b_last[:, :, 0, :, None]) * s
             + jnp.einsum('bhcd,bhce->bhde', kc * jnp.exp(b_last - b), vc))
        return s, o

    s, o = lax.scan(step, s0, tuple(chunks(t) for t in (q, k, v, log_a)))
    if not emit:
        return s, None
    o = o.transpose(1, 0, 3, 2, 4).reshape(B, T, H, GLA_VAL_DIM)
    return s, (jnp.flip(o, axis=1) if reverse else o)


def _gla_output(o, og, gain):
    B, T = og.shape[:2]
    on = o * lax.rsqrt(jnp.mean(o * o, axis=-1, keepdims=True) + GLA_NORM_EPS) * gain.astype(jnp.float32)
    return on.reshape(B, T, GLA_WIDTH) * jax.nn.silu(og.astype(jnp.float32))


def _token_mixers(p_ctx, p_lat, rows, emit_ctx, mu, w0, w2, a0, a2, g2, k_k, k_a, r_k, ln_w, ln_b,
                  gla_a2, gla_ab, gla_g):
    B = p_lat.shape[0]
    pr_c, pg_c = p_ctx[..., :RW_COLS], p_ctx[..., RW_COLS:]
    pr_l, pg_l = p_lat[..., :RW_COLS], p_lat[..., RW_COLS:]

    fc = _rwkv_inputs(pr_c, _qshift_seq(pr_c), mu, w0, w2, a0, a2, k_k, k_a)
    fl = _rwkv_inputs(pr_l, _qshift_grid(pr_l, rows), mu, w0, w2, a0, a2, k_k, k_a)
    s0 = jnp.zeros((B, RW_HEADS, RW_HEAD_DIM, RW_HEAD_DIM), jnp.float32)
    yr_c = yr_l = 0.0
    for d in range(2):
        rev = d == 1
        s_c, y_c = _rwkv7_scan(s0, fc['r'], fc['w'][d], fc['k'][d], fc['v'], fc['kk'], fc['a'][d], rev, emit_ctx)
        _, y_l = _rwkv7_scan(s_c, fl['r'], fl['w'][d], fl['k'][d], fl['v'], fl['kk'], fl['a'][d], rev, True)
        yr_l = yr_l + y_l
        if emit_ctx:
            yr_c = yr_c + y_c

    gc = _gla_inputs(pg_c, gla_a2, gla_ab)
    gl = _gla_inputs(pg_l, gla_a2, gla_ab)
    col = lambda t: _grid_transpose(t, rows, GRID_W)
    q_l, k_l, v_l = col(gl['q']), col(gl['k']), col(gl['v'])
    s0g = jnp.zeros((B, GLA_HEADS, GLA_KEY_DIM, GLA_VAL_DIM), jnp.float32)
    og_c = og_l = 0.0
    for d in range(2):
        rev = d == 1
        s_c, o_c = _gla_scan(s0g, gc['q'], gc['k'], gc['v'], gc['log_a'][d], rev, emit_ctx)
        _, o_l = _gla_scan(s_c, q_l, k_l, v_l, col(gl['log_a'][d]), rev, True)
        og_l = og_l + o_l
        if emit_ctx:
            og_c = og_c + o_c
    og_l = _grid_transpose(og_l, GRID_W, rows)

    m_lat = jnp.concatenate([_rwkv_output(yr_l, fl, r_k, ln_w, ln_b, g2),
                             _gla_output(og_l, gl['og'], gla_g)], axis=-1)
    m_ctx = None
    if emit_ctx:
        m_ctx = jnp.concatenate([_rwkv_output(yr_c, fc, r_k, ln_w, ln_b, g2),
                                 _gla_output(og_c, gc['og'], gla_g)], axis=-1)
    return m_ctx, m_lat


def setup_inputs(seed: int = 0) -> dict:
    key = jax.random.key(seed)
    ks = jax.random.split(key, 32)
    nrm = lambda k, shape, scale: jax.random.normal(k, shape, jnp.float32) * scale
    L, D = DEPTH, D_MODEL
    return {
        "x": nrm(ks[0], (BATCH, SEQ, D), 1.0),
        "c": nrm(ks[1], (BATCH, D), 1.0),
        "ctx": nrm(ks[2], (BATCH, CTX_LEN, D), 1.0),
        "c_ctx": nrm(ks[3], (D,), 1.0),
        "w_mod": nrm(ks[4], (L, D, 6 * D), 0.5 * D ** -0.5),
        "b_mod": nrm(ks[5], (L, 6 * D), 0.02),
        "norm1_g": 1.0 + nrm(ks[6], (L, D), 0.02),
        "w_in": nrm(ks[7], (L, D, N_IN), D ** -0.5),
        "rwkv_mu": jax.random.uniform(ks[8], (L, RW_COLS), jnp.float32),
        "rwkv_w0": jax.random.uniform(ks[9], (L, 2, RW_WIDTH), jnp.float32, -5.0, 0.0),
        "rwkv_w2": nrm(ks[10], (L, 2, RW_DECAY_RANK, RW_WIDTH), 0.5 * RW_DECAY_RANK ** -0.5),
        "rwkv_a0": nrm(ks[11], (L, 2, RW_WIDTH), 0.5),
        "rwkv_a2": nrm(ks[12], (L, 2, RW_ICLR_RANK, RW_WIDTH), 0.5 * RW_ICLR_RANK ** -0.5),
        "rwkv_g2": nrm(ks[13], (L, RW_GATE_RANK, RW_WIDTH), RW_GATE_RANK ** -0.5),
        "rwkv_k_k": 0.85 + nrm(ks[14], (L, RW_WIDTH), 0.02),
        "rwkv_k_a": 1.0 + nrm(ks[15], (L, RW_WIDTH), 0.02),
        "rwkv_r_k": nrm(ks[16], (L, RW_HEADS, RW_HEAD_DIM), 0.1),
        "rwkv_ln_w": 1.0 + nrm(ks[17], (L, RW_WIDTH), 0.02),
        "rwkv_ln_b": nrm(ks[18], (L, RW_WIDTH), 0.02),
        "gla_a2": nrm(ks[19], (L, 2, GLA_GATE_RANK, GLA_QK_WIDTH), GLA_GATE_RANK ** -0.5),
        "gla_ab": jax.random.uniform(ks[20], (L, 2, GLA_QK_WIDTH), jnp.float32, 0.0, 3.0),
        "gla_norm_g": 1.0 + nrm(ks[21], (L, GLA_VAL_DIM), 0.02),
        "w_out": nrm(ks[22], (L, MIX_WIDTH, D), MIX_WIDTH ** -0.5),
        "norm2_g": 1.0 + nrm(ks[23], (L, D), 0.02),
        "w_ffn_in": nrm(ks[24], (L, D, 2 * FFN_HIDDEN), D ** -0.5),
        "w_ffn_out": nrm(ks[25], (L, FFN_HIDDEN, D), FFN_HIDDEN ** -0.5),
        "norm_f_g": 1.0 + nrm(ks[26], (D,), 0.02),
    }


def reference(x, c, ctx, c_ctx, w_mod, b_mod, norm1_g, w_in, rwkv_mu, rwkv_w0, rwkv_w2, rwkv_a0, rwkv_a2,
              rwkv_g2, rwkv_k_k, rwkv_k_a, rwkv_r_k, rwkv_ln_w, rwkv_ln_b, gla_a2, gla_ab, gla_norm_g,
              w_out, norm2_g, w_ffn_in, w_ffn_out, norm_f_g):
    B, T, _ = x.shape
    rows = T // GRID_W
    silu_c = jax.nn.silu(c)
    silu_cc = jax.nn.silu(c_ctx)
    h, hc = x, ctx
    for l in range(DEPTH):
        emit_ctx = l < DEPTH - 1
        sh1, sc1, gt1, sh2, sc2, gt2 = jnp.split((silu_c @ w_mod[l] + b_mod[l])[:, None, :], 6, axis=-1)
        ch1, cc1, cg1, ch2, cc2, cg2 = jnp.split(silu_cc @ w_mod[l] + b_mod[l], 6, axis=-1)
        p_lat = _modulate(_rmsnorm(h, norm1_g[l]), sh1, sc1) @ w_in[l]
        p_ctx = _modulate(_rmsnorm(hc, norm1_g[l]), ch1, cc1) @ w_in[l]
        m_ctx, m_lat = _token_mixers(p_ctx, p_lat, rows, emit_ctx, rwkv_mu[l], rwkv_w0[l], rwkv_w2[l],
                                     rwkv_a0[l], rwkv_a2[l], rwkv_g2[l], rwkv_k_k[l], rwkv_k_a[l],
                                     rwkv_r_k[l], rwkv_ln_w[l], rwkv_ln_b[l], gla_a2[l], gla_ab[l],
                                     gla_norm_g[l])
        h = h + gt1 * (m_lat @ w_out[l])
        h = h + gt2 * _swiglu(_modulate(_rmsnorm(h, norm2_g[l]), sh2, sc2), w_ffn_in[l], w_ffn_out[l])
        if emit_ctx:
            hc = hc + cg1 * (m_ctx @ w_out[l])
            hc = hc + cg2 * _swiglu(_modulate(_rmsnorm(hc, norm2_g[l]), ch2, cc2), w_ffn_in[l], w_ffn_out[l])
    return _rmsnorm(h, norm_f_g).astype(x.dtype)
```

```python
import functools

import numpy as np
import jax
import jax.numpy as jnp
from jax import lax
from jax.experimental import pallas as pl
from jax.experimental.pallas import tpu as pltpu

F32 = jnp.float32
BF16 = jnp.bfloat16
HIGHEST = lax.Precision.HIGHEST

SUB = 8
LANES = 128

GRID_W = 64
NORM_EPS = 1e-6

RW_H = 8
RW_N = 64
RW_W = RW_H * RW_N
RW_LOW = 384
RW_COLS = 3 * RW_W + RW_LOW
RW_GN_EPS = 64e-5

GLA_H = 4
GLA_DK = 64
GLA_DV = 128
GLA_QK = GLA_H * GLA_DK
GLA_W = GLA_H * GLA_DV
GLA_RANK = 16
GLA_TAU = 16.0
GLA_C = 64
GLA_L = 16
GLA_EPS = 1e-5

G_COLS = 2 * GLA_QK + 2 * GLA_W + LANES
W_ALL = RW_COLS + G_COLS

SCAN_TB = 32
CHAIN_TB = 128
VMEM_LIMIT = 56 * 1024 * 1024


def _cp(*sem):
    return pltpu.CompilerParams(dimension_semantics=sem, vmem_limit_bytes=VMEM_LIMIT)


def _sigmoid(z):
    return 1.0 / (1.0 + jnp.exp(-z))


def _softplus(z):
    return jnp.maximum(z, 0.0) + jnp.log(1.0 + jnp.exp(-jnp.abs(z)))


def _silu(z):
    return z * _sigmoid(z)


def _mod_kernel(c_ref, w_ref, b_ref, o_ref):
    o_ref[...] = jnp.dot(_silu(c_ref[...]), w_ref[...], precision=HIGHEST,
                         preferred_element_type=F32) + b_ref[...]


def _mod(cs, w, b):
    m, d = cs.shape
    n = w.shape[1]
    tn = 512
    return pl.pallas_call(
        _mod_kernel,
        grid=(n // tn,),
        in_specs=[pl.BlockSpec((m, d), lambda j: (0, 0)),
                  pl.BlockSpec((d, tn), lambda j: (0, j)),
                  pl.BlockSpec((1, tn), lambda j: (0, j))],
        out_specs=pl.BlockSpec((m, tn), lambda j: (0, j)),
        out_shape=jax.ShapeDtypeStruct((m, n), F32),
        compiler_params=_cp("arbitrary"),
        name="mod",
    )(cs, w, b)


def _proj_kernel(x_ref, sh_ref, sc_ref, g_ref, w_ref, a2_ref, ab_ref,
                 prw_ref, og_ref, q_ref, k_ref, v_ref, gaf_ref, gab_ref, *scratch, col_tile):
    d = x_ref.shape[-1]
    x = x_ref[0].reshape(-1, d)
    tm = x.shape[0]
    ms = jnp.mean(x * x, axis=-1, keepdims=True)
    xn = x * lax.rsqrt(ms + NORM_EPS) * g_ref[...]
    xm = xn * (1.0 + sc_ref[0]) + sh_ref[0]
    xb = xm.astype(BF16)
    o = RW_COLS
    o_og = o + 2 * GLA_QK + GLA_W
    prw_ref[0] = jnp.dot(xb, w_ref[:, 0:o], preferred_element_type=F32).reshape(prw_ref.shape[1:])
    og_ref[0] = jnp.dot(xb, w_ref[:, o_og:o_og + GLA_W], preferred_element_type=F32).reshape(og_ref.shape[1:])
    if col_tile:
        xs_ref = scratch[0]
        n_slab = d // LANES
        for sl in range(n_slab):
            xs_ref[sl] = xm[:, sl * LANES:(sl + 1) * LANES]
        xb = jnp.concatenate(
            [jnp.concatenate([xs_ref[sl, pl.ds(cl, tm // SUB, stride=SUB), :] for sl in range(n_slab)], axis=1)
             for cl in range(SUB)], axis=0).astype(BF16)

    def mm(lo, hi):
        return jnp.dot(xb, w_ref[:, lo:hi], preferred_element_type=F32)

    def put(ref, val):
        ref[0] = val.reshape(ref.shape[1:])

    put(q_ref, mm(o, o + GLA_QK) * (GLA_DK ** -0.5))
    put(k_ref, mm(o + GLA_QK, o + 2 * GLA_QK))
    put(v_ref, mm(o + 2 * GLA_QK, o_og))
    lr = mm(W_ALL - LANES, W_ALL)
    z = jnp.dot(lr, a2_ref[...], precision=HIGHEST, preferred_element_type=F32) + ab_ref[...]
    la = -_softplus(-z) * (1.0 / GLA_TAU)
    put(gaf_ref, la[:, :GLA_QK])
    put(gab_ref, la[:, GLA_QK:])


def _proj(x, sh, sc, g1, w_all, a2p, abp, col_tile):
    bsz, t, d = x.shape
    per_b = sh.shape[0] > 1
    mod_spec = pl.BlockSpec((1, 1, d), (lambda b, j: (b, 0, 0)) if per_b else (lambda b, j: (0, 0, 0)))
    nat_w = (RW_COLS, GLA_W)
    gla_w = (GLA_QK, GLA_QK, GLA_W, GLA_QK, GLA_QK)
    if col_tile:
        rows = t // GRID_W
        x_in = x.reshape(bsz, rows, GRID_W, d)
        grid = (bsz, GRID_W // SUB)
        x_spec = pl.BlockSpec((1, rows, SUB, d), lambda b, j: (b, 0, j, 0))
        nat_specs = [pl.BlockSpec((1, rows, SUB, w), lambda b, j: (b, 0, j, 0)) for w in nat_w]
        nat_shapes = [jax.ShapeDtypeStruct((bsz, rows, GRID_W, w), F32) for w in nat_w]
        gla_specs = [pl.BlockSpec((1, SUB, rows, w), lambda b, j: (b, j, 0, 0)) for w in gla_w]
        gla_shapes = [jax.ShapeDtypeStruct((bsz, GRID_W, rows, w), F32) for w in gla_w]
        scratch = [pltpu.VMEM((d // LANES, rows * SUB, LANES), F32)]
    else:
        x_in = x
        grid = (bsz, 1)
        x_spec = pl.BlockSpec((1, t, d), lambda b, j: (b, 0, 0))
        nat_specs = [pl.BlockSpec((1, t, w), lambda b, j: (b, 0, 0)) for w in nat_w]
        nat_shapes = [jax.ShapeDtypeStruct((bsz, t, w), F32) for w in nat_w]
        gla_specs = [pl.BlockSpec((1, t // GLA_C, GLA_C, w), lambda b, j: (b, 0, 0, 0)) for w in gla_w]
        gla_shapes = [jax.ShapeDtypeStruct((bsz, t // GLA_C, GLA_C, w), F32) for w in gla_w]
        scratch = []
    res = pl.pallas_call(
        functools.partial(_proj_kernel, col_tile=col_tile),
        grid=grid,
        in_specs=[x_spec, mod_spec, mod_spec,
                  pl.BlockSpec((1, d), lambda b, j: (0, 0)),
                  pl.BlockSpec((d, W_ALL), lambda b, j: (0, 0), pipeline_mode=pl.Buffered(1)),
                  pl.BlockSpec((LANES, 2 * GLA_QK), lambda b, j: (0, 0)),
                  pl.BlockSpec((1, 2 * GLA_QK), lambda b, j: (0, 0))],
        out_specs=nat_specs + gla_specs,
        out_shape=nat_shapes + gla_shapes,
        scratch_shapes=scratch,
        compiler_params=_cp("arbitrary", "arbitrary"),
        name="proj",
    )(x_in, sh, sc, g1, w_all, a2p, abp)
    return [res[0].reshape(bsz, t, RW_COLS), res[1].reshape(bsz, t, GLA_W)] + list(res[2:])


def _head_sum(z):
    s = z[:, 0:LANES] + z[:, LANES:2 * LANES] + z[:, 2 * LANES:3 * LANES] + z[:, 3 * LANES:4 * LANES]
    for sh in (64, 32, 16, 8):
        s = s + pltpu.roll(s, sh, axis=1)
    return jnp.concatenate([s, s, s, s], axis=1)


def _rw_feat_kernel(*refs, grid_mode, emit):
    if grid_mode:
        p_ref, up_ref, dn_ref = refs[:3]
        refs = refs[3:]
    else:
        p_ref = refs[0]
        refs = refs[1:]
    (st_ref, mu_ref, w2_ref, w0_ref, a2_ref, a0_ref, kk_ref, ka_ref) = refs[:8]
    refs = refs[8:]
    if emit:
        rk_ref, g2_ref = refs[:2]
        refs = refs[2:]
    (nkk_o, r_o, v_o, wf_o, wb_o, kaf_o, kab_o, kdf_o, kdb_o) = refs[:9]

    p = p_ref[0]
    tq = p.shape[0]
    st = st_ref[...]
    ridx = lax.broadcasted_iota(jnp.int32, p.shape, 0)
    prev_tok = pltpu.roll(p, 1, axis=0)
    next_tok = pltpu.roll(p, tq - 1, axis=0)
    if grid_mode:
        j = pl.program_id(1)
        last = pl.num_programs(1) - 1
        col = ridx & (GRID_W - 1)
        left = jnp.where(col == 0, 0.0, prev_tok)
        right = jnp.where(col == GRID_W - 1, 0.0, next_tok)
        ext = jnp.concatenate([up_ref[0], p, dn_ref[0]], axis=0)
        up = jnp.where((ridx < GRID_W) & (j == 0), 0.0, ext[0:tq])
        down = jnp.where((ridx >= tq - GRID_W) & (j == last), 0.0, ext[2 * GRID_W:2 * GRID_W + tq])
        ps = jnp.where(st == 0, left, jnp.where(st == 1, right, jnp.where(st == 2, up, down)))
    else:
        prev_tok = jnp.where(ridx == 0, 0.0, prev_tok)
        next_tok = jnp.where(ridx == tq - 1, 0.0, next_tok)
        ps = jnp.where((st & 1) == 0, prev_tok, next_tok)
    xm = p + (ps - p) * mu_ref[...]

    r = xm[:, 0:RW_W]
    k = xm[:, RW_W:2 * RW_W]
    v = xm[:, 2 * RW_W:3 * RW_W]
    lw = xm[:, 3 * RW_W:3 * RW_W + LANES]
    la = xm[:, 3 * RW_W + LANES:3 * RW_W + 2 * LANES]
    kk = k * kk_ref[...]
    inv = 1.0 / jnp.maximum(jnp.sqrt(_head_sum(kk * kk)), 1e-12)
    kk = kk * inv
    wl = jnp.dot(jnp.tanh(lw), w2_ref[...], precision=HIGHEST, preferred_element_type=F32) + w0_ref[...]
    w = jnp.exp(-jnp.exp(-_softplus(-wl) - 0.5))
    a = _sigmoid(jnp.dot(la, a2_ref[...], precision=HIGHEST, preferred_element_type=F32) + a0_ref[...])
    ka = ka_ref[...]
    a_f = a[:, :RW_W]
    a_b = a[:, RW_W:]
    nkk_o[0] = -kk
    r_o[0] = r
    v_o[0] = v
    wf_o[0] = w[:, :RW_W]
    wb_o[0] = w[:, RW_W:]
    kaf_o[0] = kk * a_f
    kab_o[0] = kk * a_b
    kdf_o[0] = k * (1.0 + (a_f - 1.0) * ka)
    kdb_o[0] = k * (1.0 + (a_b - 1.0) * ka)
    if emit:
        bonus_o, g_o = refs[9:11]
        ksum = k * (2.0 + (a_f + a_b - 2.0) * ka)
        bonus_o[0] = _head_sum(r * ksum * rk_ref[...]) * v
        lg = xm[:, 3 * RW_W + 2 * LANES:]
        g_o[0] = jnp.dot(_sigmoid(lg), g2_ref[...], precision=HIGHEST, preferred_element_type=F32)


def _rw_feat(p, consts, emit_consts, grid_mode, tq):
    bsz, t, _ = p.shape
    emit = emit_consts is not None
    tok = pl.BlockSpec((1, tq, RW_COLS), lambda b, j: (b, j, 0))
    in_specs = [tok]
    args = [p]
    if grid_mode:
        rpb = tq // GRID_W
        nrow = t // GRID_W
        in_specs += [pl.BlockSpec((1, GRID_W, RW_COLS), lambda b, j: (b, jnp.maximum(j * rpb - 1, 0), 0)),
                     pl.BlockSpec((1, GRID_W, RW_COLS), lambda b, j: (b, jnp.minimum(j * rpb + rpb, nrow - 1), 0))]
        args += [p, p]
    cs = list(consts) + (list(emit_consts) if emit else [])
    for c in cs:
        in_specs.append(pl.BlockSpec(c.shape, lambda b, j: (0, 0)))
        args.append(c)
    n_out = 11 if emit else 9
    out_spec = pl.BlockSpec((1, tq, RW_W), lambda b, j: (b, j, 0))
    return pl.pallas_call(
        functools.partial(_rw_feat_kernel, grid_mode=grid_mode, emit=emit),
        grid=(bsz, t // tq),
        in_specs=in_specs,
        out_specs=[out_spec] * n_out,
        out_shape=[jax.ShapeDtypeStruct((bsz, t, RW_W), F32)] * n_out,
        compiler_params=_cp("arbitrary", "arbitrary"),
        name="rw_feat",
    )(*args)


def _to_chain_kernel(ff_ref, fb_ref, o_ref, z_ref):
    nb = ff_ref.shape[0]
    for d, ref in enumerate((ff_ref, fb_ref)):
        for b in range(nb):
            f = ref[b]
            if d == 1:
                f = jnp.concatenate([f[i:i + SCAN_TB] for i in range(CHAIN_TB - SCAN_TB, -1, -SCAN_TB)], axis=0)
            ft = f.T
            for n in range(RW_N):
                z_ref[pl.ds(n * LANES + d * (LANES // 2) + b * RW_H, RW_H), :] = ft[n * RW_H:(n + 1) * RW_H, :]
    for n in range(RW_N):
        blk = z_ref[pl.ds(n * LANES, LANES), :].T
        for th in range(CHAIN_TB // SUB):
            o_ref[th, n] = blk[th * SUB:(th + 1) * SUB, :]


def _to_chain(ff, fb):
    bsz, t, _ = ff.shape
    nb = t // CHAIN_TB
    return pl.pallas_call(
        _to_chain_kernel,
        grid=(nb,),
        in_specs=[pl.BlockSpec((bsz, CHAIN_TB, RW_W), lambda j: (0, j, 0)),
                  pl.BlockSpec((bsz, CHAIN_TB, RW_W), lambda j: (0, nb - 1 - j, 0))],
        out_specs=pl.BlockSpec((CHAIN_TB // SUB, RW_N, SUB, LANES), lambda j: (j, 0, 0, 0)),
        out_shape=jax.ShapeDtypeStruct((t // SUB, RW_N, SUB, LANES), F32),
        scratch_shapes=[pltpu.VMEM((RW_N * LANES, CHAIN_TB), F32)],
        compiler_params=_cp("arbitrary"),
        name="to_chain",
    )(ff, fb)


def _from_chain_kernel(yf_ref, yb_ref, o_ref, z_ref):
    nb = o_ref.shape[0]
    fwd_lane = lax.broadcasted_iota(jnp.int32, (SUB, LANES), 1) < (LANES // 2)
    for v in range(RW_N):
        blk = jnp.concatenate([jnp.where(fwd_lane, yf_ref[th, v], yb_ref[th, v])
                               for th in range(CHAIN_TB // SUB)], axis=0)
        bt = blk.T
        for b in range(nb):
            lo = b * RW_H
            hi = LANES // 2 + b * RW_H
            z_ref[pl.ds(b * RW_W + v * RW_H, RW_H), :] = bt[lo:lo + RW_H, :] + bt[hi:hi + RW_H, :]
    for b in range(nb):
        o_ref[b] = z_ref[pl.ds(b * RW_W, RW_W), :].T


def _from_chain(yf, yb, bsz):
    t = yf.shape[0] * SUB
    nb = t // CHAIN_TB
    spec = pl.BlockSpec((CHAIN_TB // SUB, RW_N, SUB, LANES), lambda j: (j, 0, 0, 0))
    return pl.pallas_call(
        _from_chain_kernel,
        grid=(nb,),
        in_specs=[spec, spec],
        out_specs=pl.BlockSpec((bsz, CHAIN_TB, RW_W), lambda j: (0, j, 0)),
        out_shape=jax.ShapeDtypeStruct((bsz, t, RW_W), F32),
        scratch_shapes=[pltpu.VMEM((bsz * RW_W, CHAIN_TB), F32)],
        compiler_params=_cp("arbitrary"),
        name="from_chain",
    )(yf, yb)


def _rwkv_scan_kernel(*refs, emit):
    s0_ref, kk_ref, w_ref, kka_ref, kd_ref, r_ref, v_ref = refs[:7]
    if emit:
        yf_ref, yb_ref, st_out_ref, s_ref = refs[7:]
    else:
        st_out_ref, s_ref = refs[7:]
    j = pl.program_id(0)

    @pl.when(j == 0)
    def _():
        s_ref[...] = s0_ref[...]

    tb = kk_ref.shape[0] // RW_N
    fwd_lane = lax.broadcasted_iota(jnp.int32, (SUB, LANES), 1) < (LANES // 2)

    def step(i, carry):
        ib = tb - 1 - i
        base_f = (i >> 3) * (RW_N * SUB) + (i & 7)
        base_b = (ib >> 3) * (RW_N * SUB) + (ib & 7)

        def tile(ref, a):
            f = ref[pl.ds(base_f + a * (SUB * SUB), SUB, stride=SUB), :]
            b = ref[pl.ds(base_b + a * (SUB * SUB), SUB, stride=SUB), :]
            return jnp.where(fwd_lane, f, b)

        def full(ref):
            return jnp.concatenate([tile(ref, a) for a in range(RW_N // SUB)], axis=0)

        nkk = full(kk_ref)
        w = full(w_ref)
        kka = full(kka_ref)
        kd = full(kd_ref)
        r = full(r_ref)
        for v in range(RW_N):
            s = s_ref[v]
            sa = jnp.sum(s * nkk, axis=0, keepdims=True)
            vf = v_ref[pl.ds(base_f + v * SUB, 1), :]
            vb = v_ref[pl.ds(base_b + v * SUB, 1), :]
            vrow = jnp.where(fwd_lane[:1], vf, vb)
            s = s * w + sa * kka + vrow * kd
            s_ref[v] = s
            if emit:
                y = jnp.sum(s * r, axis=0, keepdims=True)
                yf_ref[pl.ds(base_f + v * SUB, 1), :] = y
                yb_ref[pl.ds(base_b + v * SUB, 1), :] = y
        return carry

    lax.fori_loop(0, tb, step, 0)

    @pl.when(j == pl.num_programs(0) - 1)
    def _():
        st_out_ref[...] = s_ref[...]


def _rwkv_scan(s0, ops, emit):
    n_rows = ops[0].shape[0]
    tb = SCAN_TB
    nb = n_rows // (tb * RW_N)
    op_spec = pl.BlockSpec((tb * RW_N, LANES), lambda j: (j, 0))
    yb_spec = pl.BlockSpec((tb * RW_N, LANES), lambda j: (nb - 1 - j, 0))
    s_spec = pl.BlockSpec((RW_N, RW_N, LANES), lambda j: (0, 0, 0))
    y_sds = jax.ShapeDtypeStruct((n_rows, LANES), F32)
    s_sds = jax.ShapeDtypeStruct((RW_N, RW_N, LANES), F32)
    out_specs = ([op_spec, yb_spec] if emit else []) + [s_spec]
    out_shape = ([y_sds, y_sds] if emit else []) + [s_sds]
    return pl.pallas_call(
        functools.partial(_rwkv_scan_kernel, emit=emit),
        grid=(nb,),
        in_specs=[s_spec] + [op_spec] * 6,
        out_specs=out_specs,
        out_shape=out_shape,
        scratch_shapes=[pltpu.VMEM((RW_N, RW_N, LANES), F32)],
        compiler_params=_cp("arbitrary"),
        name="rwkv_scan",
    )(s0, *ops)


def _gla_dir(q_ref, k_ref, v_ref, g_ref, st_ref, o_ref, b_scr, tri, hm_k, hm_v, bd, pall_ref, rev, emit):
    g = g_ref[0, 0]
    b = jnp.dot(tri, g, precision=HIGHEST, preferred_element_type=F32)
    last = 0 if rev else GLA_C - 1
    b_last = b[last:last + 1, :]
    k = k_ref[0, 0]
    v = v_ref[0, 0]
    kt = (k * jnp.exp(b_last - b)).astype(BF16)
    upd = lax.dot_general(v.astype(BF16), kt, (((0,), (0,)), ((), ())), preferred_element_type=F32)
    st_old = st_ref[...]
    if emit:
        q = q_ref[0, 0]
        b_scr[...] = b
        nt = (((1,), (1,)), ((), ()))
        o = lax.dot_general((q * jnp.exp(b)).astype(BF16), st_old.astype(BF16), nt,
                            preferred_element_type=F32)
        ri = lax.broadcasted_iota(jnp.int32, (GLA_C, GLA_C), 0)
        ci = lax.broadcasted_iota(jnp.int32, (GLA_C, GLA_C), 1)
        sub_r = ri >> 4
        terms = []
        for jj in range(GLA_L):
            kj = jnp.concatenate([jnp.broadcast_to(k_ref[0, 0, pl.ds(GLA_L * s + jj, 1), :], (GLA_L, GLA_QK))
                                  for s in range(GLA_C // GLA_L)], axis=0)
            bj = jnp.concatenate([jnp.broadcast_to(b_scr[pl.ds(GLA_L * s + jj, 1), :], (GLA_L, GLA_QK))
                                  for s in range(GLA_C // GLA_L)], axis=0)
            terms.append((q * kj * jnp.exp(jnp.minimum(b - bj, 0.0))).astype(BF16))
        dmat = jnp.dot(jnp.concatenate(terms, axis=1), pall_ref[...], preferred_element_type=F32)
        ii = ri & (GLA_L - 1)
        jj_c = ci & (GLA_L - 1)
        dmask = (jj_c >= ii) if rev else (jj_c <= ii)
        dmat = jnp.where(dmask, dmat, 0.0)
        for s in range(GLA_C // GLA_L):
            lo = GLA_L * s
            has_off = (s > 0) if rev else (s < GLA_C // GLA_L - 1)
            a_s = jnp.where(sub_r == s, dmat, 0.0)
            if has_off:
                end = lo if rev else lo + GLA_L - 1
                e_s = b[end:end + 1, :]
                ks = k[lo:lo + GLA_L] * jnp.exp(e_s - b[lo:lo + GLA_L])
                kbd = (jnp.concatenate([ks] * GLA_H, axis=0) * hm_k).astype(BF16)
                qs = (q * jnp.exp(jnp.minimum(b - e_s, 0.0))).astype(BF16)
                off = lax.dot_general(qs, kbd, nt, preferred_element_type=F32)
                omask = (sub_r < s) if rev else (sub_r > s)
                a_s = jnp.where(omask, off, a_s)
            vbd = (jnp.concatenate([v[lo:lo + GLA_L]] * GLA_H, axis=0) * hm_v).astype(BF16)
            o = o + jnp.dot(a_s.astype(BF16), vbd, preferred_element_type=F32)
        o_ref[0, 0] = o
    st_ref[...] = st_old * jnp.exp(b_last) + upd * bd


def _gla_kernel(*refs, emit):
    (qf, kf, vf, gf, qb, kb, vb, gb, s0_ref, trif_ref, trib_ref, hmk_ref, hmv_ref, bd_ref, pall_ref) = refs[:15]
    if emit:
        of_ref, ob_ref, st_out_ref, st_ref, b_scr = refs[15:]
    else:
        of_ref = ob_ref = None
        st_out_ref, st_ref, b_scr = refs[15:]
    c = pl.program_id(1)

    @pl.when(c == 0)
    def _():
        st_ref[...] = s0_ref[0]

    hm_k = hmk_ref[...]
    hm_v = hmv_ref[...]
    bd = bd_ref[...]
    _gla_dir(qf, kf, vf, gf, st_ref.at[0], of_ref, b_scr.at[0], trif_ref[...], hm_k, hm_v, bd, pall_ref, False, emit)
    _gla_dir(qb, kb, vb, gb, st_ref.at[1], ob_ref, b_scr.at[1], trib_ref[...], hm_k, hm_v, bd, pall_ref, True, emit)

    @pl.when(c == pl.num_programs(1) - 1)
    def _():
        st_out_ref[0] = st_ref[...]


def _gla_consts():
    ri = np.arange(GLA_C)[:, None]
    ci = np.arange(GLA_C)[None, :]
    tri_f = (ci <= ri).astype(np.float32)
    tri_b = (ci >= ri).astype(np.float32)
    row_h = np.arange(GLA_C)[:, None] // GLA_L
    hm_k = (row_h == (np.arange(GLA_QK)[None, :] // GLA_DK)).astype(np.float32)
    hm_v = (row_h == (np.arange(GLA_W)[None, :] // GLA_DV)).astype(np.float32)
    bd = ((np.arange(GLA_W)[:, None] // GLA_DV) == (np.arange(GLA_QK)[None, :] // GLA_DK)).astype(np.float32)
    pall = np.zeros((GLA_L, GLA_QK, GLA_C), np.float32)
    for jj in range(GLA_L):
        for h in range(GLA_H):
            pall[jj, h * GLA_DK:(h + 1) * GLA_DK, h * GLA_L + jj] = 1.0
    pall = pall.reshape(GLA_L * GLA_QK, GLA_C)
    return (jnp.asarray(tri_f), jnp.asarray(tri_b), jnp.asarray(hm_k), jnp.asarray(hm_v), jnp.asarray(bd),
            jnp.asarray(pall, dtype=BF16))


def _gla(q, k, v, gf, gb, s0, emit):
    bsz, nc = q.shape[:2]

    def spec(width, rev):
        return pl.BlockSpec((1, 1, GLA_C, width),
                            (lambda b, c: (b, nc - 1 - c, 0, 0)) if rev else (lambda b, c: (b, c, 0, 0)))

    consts = _gla_consts()
    ins = [q, k, v, gf, q, k, v, gb, s0, *consts]
    in_specs = [spec(GLA_QK, False), spec(GLA_QK, False), spec(GLA_W, False), spec(GLA_QK, False),
                spec(GLA_QK, True), spec(GLA_QK, True), spec(GLA_W, True), spec(GLA_QK, True),
                pl.BlockSpec((1, 2, GLA_W, GLA_QK), lambda b, c: (b, 0, 0, 0))]
    in_specs += [pl.BlockSpec(cst.shape, lambda b, c: (0, 0)) for cst in consts]
    st_sds = jax.ShapeDtypeStruct((bsz, 2, GLA_W, GLA_QK), F32)
    st_spec = pl.BlockSpec((1, 2, GLA_W, GLA_QK), lambda b, c: (b, 0, 0, 0))
    out_specs = [st_spec]
    out_shape = [st_sds]
    if emit:
        o_sds = jax.ShapeDtypeStruct(v.shape, F32)
        out_specs = [spec(GLA_W, False), spec(GLA_W, True)] + out_specs
        out_shape = [o_sds, o_sds] + out_shape
    return pl.pallas_call(
        functools.partial(_gla_kernel, emit=emit),
        grid=(bsz, nc),
        in_specs=in_specs,
        out_specs=out_specs,
        out_shape=out_shape,
        scratch_shapes=[pltpu.VMEM((2, GLA_W, GLA_QK), F32), pltpu.VMEM((2, GLA_C, GLA_QK), F32)],
        compiler_params=_cp("arbitrary", "arbitrary"),
        name="gla",
    )(*ins)


def _mix_kernel(y_ref, bonus_ref, g_ref, of_ref, ob_ref, og_ref, x_ref, gt_ref, lnw_ref, lnb_ref, gain_ref,
                wo_ref, h_ref, os_ref):
    n_slab, tm = os_ref.shape[:2]

    def nat(ref):
        return ref[0].reshape(tm, ref.shape[-1])

    y = nat(y_ref)
    mean = _head_sum(y) * (1.0 / RW_N)
    yc = y - mean
    var = _head_sum(yc * yc) * (1.0 / RW_N)
    yn = yc * lax.rsqrt(var + RW_GN_EPS) * lnw_ref[...] + lnb_ref[...]
    rw = (yn + nat(bonus_ref)) * nat(g_ref)
    for cl in range(SUB):
        oc = of_ref[0, cl] + ob_ref[0, cl]
        for sl in range(n_slab):
            os_ref[sl, pl.ds(cl, tm // SUB, stride=SUB), :] = oc[:, sl * LANES:(sl + 1) * LANES]
    gain = gain_ref[...]
    parts = []
    for h in range(GLA_H):
        oh = os_ref[h]
        ms = jnp.mean(oh * oh, axis=-1, keepdims=True)
        parts.append(oh * lax.rsqrt(ms + GLA_EPS) * gain)
    gl = jnp.concatenate(parts, axis=1) * _silu(nat(og_ref))
    m = jnp.concatenate([rw, gl], axis=1).astype(BF16)
    h = nat(x_ref) + gt_ref[0] * jnp.dot(m, wo_ref[...], preferred_element_type=F32)
    h_ref[0] = h.reshape(h_ref.shape[1:])


def _mix(y, bonus, g, o_f, o_b, og, x, gt1, lnw, lnb, gain, wo):
    bsz, t, d = x.shape
    rows = t // GRID_W

    def tok(width):
        return pl.BlockSpec((1, rows, SUB, width), lambda b, j: (b, 0, j, 0))

    def grid4(a):
        return a.reshape(bsz, rows, GRID_W, a.shape[-1])

    def const(a):
        return pl.BlockSpec(a.shape, lambda b, j: (0,) * a.ndim)

    col = pl.BlockSpec((1, SUB, rows, GLA_W), lambda b, j: (b, j, 0, 0))
    h = pl.pallas_call(
        _mix_kernel,
        grid=(bsz, GRID_W // SUB),
        in_specs=[tok(RW_W), tok(RW_W), tok(RW_W), col, col, tok(GLA_W), tok(d),
                  pl.BlockSpec((1, 1, d), lambda b, j: (b, 0, 0)),
                  const(lnw), const(lnb), const(gain), const(wo)],
        out_specs=tok(d),
        out_shape=jax.ShapeDtypeStruct((bsz, rows, GRID_W, d), F32),
        scratch_shapes=[pltpu.VMEM((GLA_W // LANES, rows * SUB, LANES), F32)],
        compiler_params=_cp("arbitrary", "arbitrary"),
        name="mix",
    )(grid4(y), grid4(bonus), grid4(g), o_f, o_b, grid4(og), grid4(x), gt1, lnw, lnb, gain, wo)
    return h.reshape(bsz, t, d)


def _ffn_kernel(h_ref, sh_ref, sc_ref, gt_ref, g2_ref, gf_ref, wi_ref, wo_ref, o_ref, *, n_chunk):
    h = h_ref[0]
    ms = jnp.mean(h * h, axis=-1, keepdims=True)
    hn = h * lax.rsqrt(ms + NORM_EPS) * g2_ref[...]
    hb = (hn * (1.0 + sc_ref[0]) + sh_ref[0]).astype(BF16)
    hid = wo_ref.shape[0]
    cw = hid // n_chunk
    acc = jnp.zeros(h.shape, F32)
    for c in range(n_chunk):
        gate = jnp.dot(hb, wi_ref[:, c * cw:(c + 1) * cw], preferred_element_type=F32)
        up = jnp.dot(hb, wi_ref[:, hid + c * cw:hid + (c + 1) * cw], preferred_element_type=F32)
        act = (_silu(gate) * up).astype(BF16)
        acc = acc + jnp.dot(act, wo_ref[c * cw:(c + 1) * cw, :], preferred_element_type=F32)
    h2 = h + gt_ref[0] * acc
    ms2 = jnp.mean(h2 * h2, axis=-1, keepdims=True)
    o_ref[0] = h2 * lax.rsqrt(ms2 + NORM_EPS) * gf_ref[...]


def _ffn(h, sh2, sc2, gt2, g2, gf, wi, wo, tm):
    bsz, t, d = h.shape
    tok = pl.BlockSpec((1, tm, d), lambda b, j: (b, j, 0))
    mod_spec = pl.BlockSpec((1, 1, d), lambda b, j: (b, 0, 0))
    vec = pl.BlockSpec((1, d), lambda b, j: (0, 0))
    hid = wo.shape[0]
    n_chunk = 2 if hid % (2 * LANES) == 0 else 1
    return pl.pallas_call(
        functools.partial(_ffn_kernel, n_chunk=n_chunk),
        grid=(bsz, t // tm),
        in_specs=[tok, mod_spec, mod_spec, mod_spec, vec, vec,
                  pl.BlockSpec(wi.shape, lambda b, j: (0, 0), pipeline_mode=pl.Buffered(1)),
                  pl.BlockSpec(wo.shape, lambda b, j: (0, 0), pipeline_mode=pl.Buffered(1))],
        out_specs=tok,
        out_shape=jax.ShapeDtypeStruct((bsz, t, d), F32),
        compiler_params=_cp("arbitrary", "arbitrary"),
        name="ffn",
    )(h, sh2, sc2, gt2, g2, gf, wi, wo)


def kernel(x, c, ctx, c_ctx, w_mod, b_mod, norm1_g, w_in, rwkv_mu, rwkv_w0, rwkv_w2, rwkv_a0, rwkv_a2, rwkv_g2, rwkv_k_k, rwkv_k_a, rwkv_r_k, rwkv_ln_w, rwkv_ln_b, gla_a2, gla_ab, gla_norm_g, w_out, norm2_g, w_ffn_in, w_ffn_out, norm_f_g):
    bsz, t, d = x.shape
    n_ctx = ctx.shape[1]
    assert w_mod.shape[0] == 1, "single-layer block"
    assert bsz * RW_H * 2 == LANES and t // GRID_W == GLA_C and t % CHAIN_TB == 0 and n_ctx % CHAIN_TB == 0

    perm = np.arange(RW_W).reshape(RW_H, RW_N).T.reshape(-1)
    stype = np.concatenate([perm % 4] * 3 + [np.arange(RW_LOW) % 4]).astype(np.int32)[None, :]

    def p3(a):
        return jnp.concatenate([a[..., 0:RW_W][..., perm], a[..., RW_W:2 * RW_W][..., perm],
                                a[..., 2 * RW_W:3 * RW_W][..., perm], a[..., 3 * RW_W:]], axis=-1)

    wi0 = w_in[0]
    wg = wi0[:, RW_COLS:]
    o1, o2, o3 = 2 * GLA_QK, 2 * GLA_QK + GLA_W, 2 * GLA_QK + GLA_W + 2 * GLA_RANK
    w_all = jnp.concatenate([p3(wi0[:, :RW_COLS]), wg[:, :o2], wg[:, o3:], wg[:, o2:o3],
                             jnp.zeros((d, LANES - 2 * GLA_RANK), F32)], axis=1).astype(BF16)
    a2p = jnp.zeros((LANES, 2 * GLA_QK), F32)
    a2p = a2p.at[0:GLA_RANK, 0:GLA_QK].set(gla_a2[0, 0]).at[GLA_RANK:2 * GLA_RANK, GLA_QK:].set(gla_a2[0, 1])
    abp = jnp.concatenate([gla_ab[0, 0], gla_ab[0, 1]])[None, :]

    def blockdiag2(m):
        z = jnp.zeros((RW_N, RW_W), F32)
        return jnp.concatenate([jnp.concatenate([m[0][:, perm], z], axis=1),
                                jnp.concatenate([z, m[1][:, perm]], axis=1)], axis=0)

    def cat2(m):
        return jnp.concatenate([m[0][perm], m[1][perm]])[None, :]

    feat_consts = (jnp.asarray(stype), p3(rwkv_mu[0])[None, :], blockdiag2(rwkv_w2[0]), cat2(rwkv_w0[0]),
                   blockdiag2(rwkv_a2[0]), cat2(rwkv_a0[0]), rwkv_k_k[0][perm][None, :], rwkv_k_a[0][perm][None, :])
    emit_consts = (rwkv_r_k[0].reshape(-1)[perm][None, :], rwkv_g2[0][:, perm])
    lnw = rwkv_ln_w[0][perm][None, :]
    lnb = rwkv_ln_b[0][perm][None, :]
    gain = gla_norm_g[0][None, :]
    wo = jnp.concatenate([w_out[0][:RW_W][perm], w_out[0][RW_W:]], axis=0).astype(BF16)
    wfi = w_ffn_in[0].astype(BF16)
    wfo = w_ffn_out[0].astype(BF16)

    m_rows = 2 * SUB
    cs = jnp.concatenate([c, c_ctx[None, :], jnp.zeros((m_rows - bsz - 1, d), F32)], axis=0)
    mod = _mod(cs, w_mod[0], b_mod[0][None, :])
    sh1, sc1, gt1, sh2, sc2, gt2 = [mod[:bsz, i * d:(i + 1) * d][:, None, :] for i in range(6)]
    ch1 = mod[bsz:bsz + 1, 0:d][:, None, :]
    cc1 = mod[bsz:bsz + 1, d:2 * d][:, None, :]

    g1 = norm1_g[0][None, :]
    p_l, og_l, q_l, k_l, v_l, gaf_l, gab_l = _proj(x, sh1, sc1, g1, w_all, a2p, abp, True)
    p_c, _, q_c, k_c, v_c, gaf_c, gab_c = _proj(ctx, ch1, cc1, g1, w_all, a2p, abp, False)

    f_c = _rw_feat(p_c, feat_consts, None, False, n_ctx)
    f_l = _rw_feat(p_l, feat_consts, emit_consts, True, 256)

    def chain_ops(f):
        nkk, r, v, wf, wb, kaf, kab, kdf, kdb = f[:9]
        pairs = ((nkk, nkk), (wf, wb), (kaf, kab), (kdf, kdb), (r, r), (v, v))
        return [_to_chain(a, b).reshape(-1, LANES) for a, b in pairs]

    s0 = jnp.zeros((RW_N, RW_N, LANES), F32)
    s_c = _rwkv_scan(s0, chain_ops(f_c), False)[0]
    yf, yb, _ = _rwkv_scan(s_c, chain_ops(f_l), True)
    y_shape = (t // SUB, RW_N, SUB, LANES)
    y_l = _from_chain(yf.reshape(y_shape), yb.reshape(y_shape), bsz)
    bonus, g_gate = f_l[9], f_l[10]

    g0 = jnp.zeros((bsz, 2, GLA_W, GLA_QK), F32)
    g_c = _gla(q_c, k_c, v_c, gaf_c, gab_c, g0, False)[0]
    o_f, o_b, _ = _gla(q_l, k_l, v_l, gaf_l, gab_l, g_c, True)

    h1 = _mix(y_l, bonus, g_gate, o_f, o_b, og_l, x, gt1, lnw, lnb, gain, wo)
    return _ffn(h1, sh2, sc2, gt2, norm2_g[0][None, :], norm_f_g[None, :], wfi, wfo, 256)
```
